```python
import jax, jax.numpy as jnp
from jax import lax
import numpy as np

D_MODEL = 1024
BATCH = 2
SEQ = 16384
DEPTH = 1

CHUNK = 64
LEFT_CHUNKS = 8
BAND_CHUNKS = LEFT_CHUNKS + 1
D_MIX = D_MODEL
D_ATT = D_MIX // 2
D_CONV = D_MIX - D_ATT
N_HEADS = 8
HEAD_DIM = D_ATT // N_HEADS
MAX_REL = 128
N_REL = 2 * MAX_REL + 1
CONV_WIDTH = 31
EPS = 1e-6
SPLITS = (D_ATT, D_ATT, D_ATT, D_ATT, D_CONV, D_CONV, D_CONV)
D_IN = sum(SPLITS)

kernel_name = "hymba_chunk_attn_conformer_conv_layer"


def rmsnorm(x, g):
    xf = x.astype(jnp.float32)
    y = xf * lax.rsqrt(jnp.mean(xf * xf, axis=-1, keepdims=True) + EPS)
    return (y * g.astype(jnp.float32)).astype(x.dtype)


def layernorm(x, g, b):
    xf = x.astype(jnp.float32)
    mu = jnp.mean(xf, axis=-1, keepdims=True)
    var = jnp.mean(jnp.square(xf - mu), axis=-1, keepdims=True)
    y = (xf - mu) * lax.rsqrt(var + EPS)
    return (y * g.astype(jnp.float32) + b.astype(jnp.float32)).astype(x.dtype)


def chunk_attention(q, k, v, rel_bias):
    b, s, _ = q.shape
    nc = s // CHUNK
    band = BAND_CHUNKS * CHUNK
    qc = q.reshape(b, nc, CHUNK, N_HEADS, HEAD_DIM)
    pad = ((0, 0), (LEFT_CHUNKS * CHUNK, 0), (0, 0))
    kp = jnp.pad(k, pad).reshape(b, nc + LEFT_CHUNKS, CHUNK, N_HEADS, HEAD_DIM)
    vp = jnp.pad(v, pad).reshape(b, nc + LEFT_CHUNKS, CHUNK, N_HEADS, HEAD_DIM)
    kb = jnp.concatenate([kp[:, i:i + nc] for i in range(BAND_CHUNKS)], axis=2)
    vb = jnp.concatenate([vp[:, i:i + nc] for i in range(BAND_CHUNKS)], axis=2)
    scale = HEAD_DIM ** -0.5
    scores = jnp.einsum('bnqhd,bnkhd->bhnqk', qc, kb).astype(jnp.float32) * scale
    qi = jnp.arange(CHUNK)[:, None]
    kj = jnp.arange(band)[None, :]
    rel = LEFT_CHUNKS * CHUNK + qi - kj
    idx = jnp.clip(rel, -MAX_REL, MAX_REL) + MAX_REL
    bias = rel_bias.astype(jnp.float32)[:, idx]
    key_chunk = jnp.arange(nc)[:, None] - LEFT_CHUNKS + (jnp.arange(band) // CHUNK)[None, :]
    valid = key_chunk >= 0
    scores = scores + bias[None, :, None, :, :]
    scores = jnp.where(valid[None, None, :, None, :], scores, jnp.float32(-1e30))
    p = jax.nn.softmax(scores, axis=-1).astype(v.dtype)
    out = jnp.einsum('bhnqk,bnkhd->bnqhd', p, vb)
    return out.reshape(b, s, N_HEADS * HEAD_DIM)


def conformer_conv(val, gate_glu, dw_w, dw_b, ln_g, ln_b, w_pw, b_pw):
    hg = val * jax.nn.sigmoid(gate_glu)
    kern = dw_w.reshape(CONV_WIDTH, 1, D_CONV).astype(hg.dtype)
    hc = lax.conv_general_dilated(
        hg, kern, window_strides=(1,), padding=[(CONV_WIDTH - 1, 0)],
        dimension_numbers=('NWC', 'WIO', 'NWC'), feature_group_count=D_CONV)
    hc = hc + dw_b
    hn = layernorm(hc, ln_g, ln_b)
    hs = jax.nn.silu(hn)
    return hs @ w_pw + b_pw


def setup_inputs(seed: int = 0) -> dict:
    key = jax.random.key(seed)
    ks = jax.random.split(key, 12)
    f = jnp.float32
    x = jax.random.normal(ks[0], (BATCH, SEQ, D_MODEL), f)
    pre_norm_g = 1.0 + 0.05 * jax.random.normal(ks[1], (DEPTH, D_MODEL), f)
    w_in = jax.random.normal(ks[2], (DEPTH, D_MODEL, D_IN), f) * D_MODEL ** -0.5
    rel_bias = 0.2 * jax.random.normal(ks[3], (DEPTH, N_HEADS, N_REL), f)
    dw_w = jax.random.normal(ks[4], (DEPTH, CONV_WIDTH, D_CONV), f) * CONV_WIDTH ** -0.5
    dw_b = 0.02 * jax.random.normal(ks[5], (DEPTH, D_CONV), f)
    conv_ln_g = 1.0 + 0.05 * jax.random.normal(ks[6], (DEPTH, D_CONV), f)
    conv_ln_b = 0.02 * jax.random.normal(ks[7], (DEPTH, D_CONV), f)
    w_pw = jax.random.normal(ks[8], (DEPTH, D_CONV, D_CONV), f) * D_CONV ** -0.5
    b_pw = 0.02 * jax.random.normal(ks[9], (DEPTH, D_CONV), f)
    w_out = jax.random.normal(ks[10], (DEPTH, D_MIX, D_MODEL), f) * D_MIX ** -0.5
    post_norm_g = 1.0 + 0.05 * jax.random.normal(ks[11], (DEPTH, D_MODEL), f)
    return {"x": x, "pre_norm_g": pre_norm_g, "w_in": w_in, "rel_bias": rel_bias,
            "dw_w": dw_w, "dw_b": dw_b, "conv_ln_g": conv_ln_g, "conv_ln_b": conv_ln_b,
            "w_pw": w_pw, "b_pw": b_pw, "w_out": w_out, "post_norm_g": post_norm_g}


def reference(x, pre_norm_g, w_in, rel_bias, dw_w, dw_b, conv_ln_g, conv_ln_b,
              w_pw, b_pw, w_out, post_norm_g):
    split_pts = list(np.cumsum(SPLITS)[:-1])
    for l in range(DEPTH):
        h = rmsnorm(x, pre_norm_g[l])
        u = h @ w_in[l]
        q, k, v, g_att, c_val, c_glu, g_conv = jnp.split(u, split_pts, axis=-1)
        a = chunk_attention(q, k, v, rel_bias[l]) * jax.nn.silu(g_att)
        c = conformer_conv(c_val, c_glu, dw_w[l], dw_b[l], conv_ln_g[l], conv_ln_b[l],
                           w_pw[l], b_pw[l]) * jax.nn.silu(g_conv)
        y = jnp.concatenate([a, c], axis=-1) @ w_out[l]
        x = x + rmsnorm(y, post_norm_g[l])
    return x
```

```python
import functools

import jax
import jax.numpy as jnp
import numpy as np
from jax import lax
from jax.experimental import pallas as pl
from jax.experimental.pallas import tpu as pltpu

CHUNK = 64
LEFT_CHUNKS = 8
N_HEADS = 8
HEAD_DIM = 64
MAX_REL = 128
CONV_WIDTH = 31
EPS = 1e-6
MASK_VALUE = -1e30
MXU_DTYPE = jnp.bfloat16

LANES = 128
VMEM_LIMIT_BYTES = 56 * 1024 * 1024

TILE = 4 * CHUNK
HIST = LEFT_CHUNKS * CHUNK
BAND = HIST + TILE
N_SLOTS = BAND // TILE
CONV_HIST = 32
HEADS_PER_SLAB = LANES // HEAD_DIM


def _rmsnorm(v, g):
    ms = jnp.mean(v * v, axis=-1, keepdims=True)
    return v * lax.rsqrt(ms + EPS) * g


def _silu(v):
    return v * jax.nn.sigmoid(v)


def _layer_kernel(x_ref, pre_g_ref, w_in_ref, bias_ref, dw_w_ref, dw_b_ref,
                  ln_g_ref, ln_b_ref, w_pw_ref, b_pw_ref, w_out_ref, post_g_ref,
                  out_ref,
                  k_ring, vt_ring, hg_buf, eff_ref, at_buf, *, d_att, d_conv):
    s = pl.program_id(1)
    f32 = jnp.float32
    bf16 = MXU_DTYPE

    cur = lax.rem(s, N_SLOTS)

    @pl.when(s == 0)
    def _():
        k_ring[...] = jnp.zeros(k_ring.shape, bf16)
        vt_ring[...] = jnp.zeros(vt_ring.shape, bf16)
        hg_buf[0:CONV_HIST, :] = jnp.zeros((CONV_HIST, d_conv), f32)

    @pl.when(s <= HIST // TILE)
    def _():
        key_pos = s * TILE - HIST + lax.broadcasted_iota(jnp.int32, (BAND, TILE), 0)
        start_mask = jnp.where(key_pos >= 0, 0.0, MASK_VALUE).astype(f32)
        for h in range(N_HEADS):
            eff_ref[h] = bias_ref[h] + start_mask

    x = x_ref[...]
    h_in = _rmsnorm(x, pre_g_ref[...]).astype(bf16)

    def proj(i, width):
        return jnp.dot(h_in, w_in_ref[:, i:i + width], preferred_element_type=f32)

    q = proj(0, d_att) * (HEAD_DIM ** -0.5)
    k = proj(d_att, d_att)
    v = proj(2 * d_att, d_att)
    k_ring[pl.ds(pl.multiple_of(cur * TILE, TILE), TILE), :] = k.astype(bf16)
    vt_ring[cur] = v.T.astype(bf16)

    lane = lax.broadcasted_iota(jnp.int32, (TILE, d_att), 1)
    head_in_slab = (lane // HEAD_DIM) % HEADS_PER_SLAB
    q_bf = q.astype(bf16)
    q_sel = [jnp.where(head_in_slab == j, q_bf, jnp.zeros_like(q_bf))
             for j in range(HEADS_PER_SLAB)]

    bias_row = [pl.multiple_of((N_SLOTS - 1 - lax.rem(cur - b + N_SLOTS, N_SLOTS)) * TILE, TILE)
                for b in range(N_SLOTS)]

    for h in range(N_HEADS):
        slab = slice((h // HEADS_PER_SLAB) * LANES, (h // HEADS_PER_SLAB + 1) * LANES)
        rows = slice(h * HEAD_DIM, (h + 1) * HEAD_DIM)
        st = lax.dot_general(k_ring[:, slab], q_sel[h % HEADS_PER_SLAB][:, slab],
                             (((1,), (1,)), ((), ())), preferred_element_type=f32)
        st = [st[b * TILE:(b + 1) * TILE] + eff_ref[h, pl.ds(bias_row[b], TILE), :]
              for b in range(N_SLOTS)]
        m = functools.reduce(jnp.maximum, [jnp.max(t, axis=0, keepdims=True) for t in st])
        p = [jnp.exp(t - m) for t in st]
        denom = sum(jnp.sum(t, axis=0, keepdims=True) for t in p)
        ot = sum(jnp.dot(vt_ring[b, rows, :], p[b].astype(bf16), preferred_element_type=f32)
                 for b in range(N_SLOTS))
        at_buf[rows, :] = ot / denom

    g_att = proj(3 * d_att, d_att)
    a = at_buf[...].T * _silu(g_att)

    c0 = 4 * d_att
    c_val = proj(c0, d_conv)
    c_glu = proj(c0 + d_conv, d_conv)
    hg_buf[CONV_HIST:CONV_HIST + TILE, :] = c_val * jax.nn.sigmoid(c_glu)
    first = CONV_HIST - (CONV_WIDTH - 1)
    conv_cols = []
    for c in range(d_conv // LANES):
        cols = slice(c * LANES, (c + 1) * LANES)
        acc = jnp.broadcast_to(dw_b_ref[:, cols], (TILE, LANES))
        for t in range(CONV_WIDTH):
            acc = acc + dw_w_ref[t:t + 1, cols] * hg_buf[first + t:first + t + TILE, cols]
        conv_cols.append(acc)
    hc = jnp.concatenate(conv_cols, axis=-1)
    mu = jnp.mean(hc, axis=-1, keepdims=True)
    var = jnp.mean(jnp.square(hc - mu), axis=-1, keepdims=True)
    hn = (hc - mu) * lax.rsqrt(var + EPS) * ln_g_ref[...] + ln_b_ref[...]
    hs = _silu(hn).astype(bf16)
    pw = jnp.dot(hs, w_pw_ref[...], preferred_element_type=f32) + b_pw_ref[...]
    g_conv = proj(c0 + 2 * d_conv, d_conv)
    cmod = pw * _silu(g_conv)

    y = (jnp.dot(a.astype(bf16), w_out_ref[0:d_att, :], preferred_element_type=f32)
         + jnp.dot(cmod.astype(bf16), w_out_ref[d_att:d_att + d_conv, :],
                   preferred_element_type=f32))
    out_ref[...] = x + _rmsnorm(y, post_g_ref[...])

    hg_buf[0:CONV_HIST, :] = hg_buf[TILE:TILE + CONV_HIST, :]


def _score_bias(rel_bias):
    r = np.arange(BAND)[:, None]
    c = np.arange(TILE)[None, :]
    rel = HIST + c - r
    idx = np.clip(rel, -MAX_REL, MAX_REL) + MAX_REL
    key_chunk = r // CHUNK
    q_chunk = c // CHUNK
    in_band = (key_chunk >= q_chunk) & (key_chunk <= q_chunk + LEFT_CHUNKS)
    bias = rel_bias.astype(jnp.float32)[:, idx]
    return jnp.where(jnp.asarray(in_band)[None], bias, jnp.float32(MASK_VALUE))


def _const_spec(shape):
    return pl.BlockSpec(shape, lambda b, s: (0,) * len(shape))


def _layer(x, pre_g, w_in, rel_bias, dw_w, dw_b, ln_g, ln_b, w_pw, b_pw, w_out, post_g):
    batch, seq, d_model = x.shape
    d_att = N_HEADS * HEAD_DIM
    d_conv = w_pw.shape[0]
    assert seq % TILE == 0 and d_att % LANES == 0 and d_conv % LANES == 0
    assert w_in.shape == (d_model, 4 * d_att + 3 * d_conv)
    assert w_out.shape == (d_att + d_conv, d_model)
    row = lambda p: p.reshape(1, -1).astype(jnp.float32)
    bf16 = MXU_DTYPE
    kernel = functools.partial(_layer_kernel, d_att=d_att, d_conv=d_conv)
    tile_spec = pl.BlockSpec((None, TILE, d_model), lambda b, s: (b, s, 0))
    return pl.pallas_call(
        kernel,
        grid=(batch, seq // TILE),
        in_specs=[
            tile_spec,
            _const_spec((1, d_model)),
            _const_spec(w_in.shape),
            _const_spec((N_HEADS, BAND, TILE)),
            _const_spec(dw_w.shape),
            _const_spec((1, d_conv)),
            _const_spec((1, d_conv)),
            _const_spec((1, d_conv)),
            _const_spec(w_pw.shape),
            _const_spec((1, d_conv)),
            _const_spec(w_out.shape),
            _const_spec((1, d_model)),
        ],
        out_specs=tile_spec,
        out_shape=jax.ShapeDtypeStruct(x.shape, x.dtype),
        scratch_shapes=[
            pltpu.VMEM((BAND, d_att), bf16),
            pltpu.VMEM((N_SLOTS, d_att, TILE), bf16),
            pltpu.VMEM((CONV_HIST + TILE, d_conv), jnp.float32),
            pltpu.VMEM((N_HEADS, BAND, TILE), jnp.float32),
            pltpu.VMEM((d_att, TILE), jnp.float32),
        ],
        compiler_params=pltpu.CompilerParams(
            dimension_semantics=("arbitrary", "arbitrary"),
            vmem_limit_bytes=VMEM_LIMIT_BYTES),
        name="hymba_layer",
    )(x, row(pre_g), w_in.astype(bf16), _score_bias(rel_bias), dw_w.astype(jnp.float32),
      row(dw_b), row(ln_g), row(ln_b), w_pw.astype(bf16), row(b_pw), w_out.astype(bf16),
      row(post_g))


def kernel(x, pre_norm_g, w_in, rel_bias, dw_w, dw_b, conv_ln_g, conv_ln_b, w_pw, b_pw,
           w_out, post_norm_g):
    for l in range(pre_norm_g.shape[0]):
        x = _layer(x, pre_norm_g[l], w_in[l], rel_bias[l], dw_w[l], dw_b[l], conv_ln_g[l],
                   conv_ln_b[l], w_pw[l], b_pw[l], w_out[l], post_norm_g[l])
    return x
```

```python
import functools

import jax
import jax.numpy as jnp
import numpy as np
from jax import lax
from jax.experimental import pallas as pl
from jax.experimental.pallas import tpu as pltpu

CHUNK = 64
LEFT_CHUNKS = 8
N_HEADS = 8
HEAD_DIM = 64
MAX_REL = 128
CONV_WIDTH = 31
EPS = 1e-6
MASK_VALUE = -1e30
MXU_DTYPE = jnp.bfloat16

LANES = 128
VMEM_LIMIT_BYTES = 56 * 1024 * 1024

TILE = 4 * CHUNK
HIST = LEFT_CHUNKS * CHUNK
BAND = HIST + TILE
N_SLOTS = BAND // TILE
CONV_HIST = 32
HEADS_PER_SLAB = LANES // HEAD_DIM


def _rmsnorm(v, g):
    ms = jnp.mean(v * v, axis=-1, keepdims=True)
    return v * lax.rsqrt(ms + EPS) * g


def _silu(v):
    return v * jax.nn.sigmoid(v)


def _layer_kernel(x_ref, pre_g_ref, w_in_ref, bias_ref, dw_w_ref, dw_b_ref,
                  ln_g_ref, ln_b_ref, w_pw_ref, b_pw_ref, w_out_ref, post_g_ref,
                  out_ref,
                  k_ring, vt_ring, hg_buf, eff_ref, at_buf, *, d_att, d_conv):
    s = pl.program_id(1)
    f32 = jnp.float32
    bf16 = MXU_DTYPE

    cur = lax.rem(s, N_SLOTS)

    @pl.when(s == 0)
    def _():
        k_ring[...] = jnp.zeros(k_ring.shape, bf16)
        vt_ring[...] = jnp.zeros(vt_ring.shape, bf16)
        hg_buf[0:CONV_HIST, :] = jnp.zeros((CONV_HIST, d_conv), f32)

    @pl.when(s <= HIST // TILE)
    def _():
        key_pos = s * TILE - HIST + lax.broadcasted_iota(jnp.int32, (BAND, TILE), 0)
        start_mask = jnp.where(key_pos >= 0, 0.0, MASK_VALUE).astype(f32)
        for h in range(N_HEADS):
            eff_ref[h] = bias_ref[h] + start_mask

    x = x_ref[...]
    h_in = _rmsnorm(x, pre_g_ref[...]).astype(bf16)

    def proj(i, width):
        return jnp.dot(h_in, w_in_ref[:, i:i + width], preferred_element_type=f32)

    q = proj(0, d_att) * (HEAD_DIM ** -0.5)
    k = proj(d_att, d_att)
    v = proj(2 * d_att, d_att)
    k_ring[pl.ds(pl.multiple_of(cur * TILE, TILE), TILE), :] = k.astype(bf16)
    vt_ring[cur] = v.T.astype(bf16)

    lane = lax.broadcasted_iota(jnp.int32, (TILE, d_att), 1)
    head_in_slab = (lane // HEAD_DIM) % HEADS_PER_SLAB
    q_bf = q.astype(bf16)
    q_sel = [jnp.where(head_in_slab == j, q_bf, jnp.zeros_like(q_bf))
             for j in range(HEADS_PER_SLAB)]

    bias_row = [pl.multiple_of((N_SLOTS - 1 - lax.rem(cur - b + N_SLOTS, N_SLOTS)) * TILE, TILE)
                for b in range(N_SLOTS)]

    for h in range(N_HEADS):
        slab = slice((h // HEADS_PER_SLAB) * LANES, (h // HEADS_PER_SLAB + 1) * LANES)
        rows = slice(h * HEAD_DIM, (h + 1) * HEAD_DIM)
        st = lax.dot_general(k_ring[:, slab], q_sel[h % HEADS_PER_SLAB][:, slab],
                             (((1,), (1,)), ((), ())), preferred_element_type=f32)
        st = [st[b * TILE:(b + 1) * TILE] + eff_ref[h, pl.ds(bias_row[b], TILE), :]
              for b in range(N_SLOTS)]
        m = functools.reduce(jnp.maximum, [jnp.max(t, axis=0, keepdims=True) for t in st])
        p = [jnp.exp(t - m) for t in st]
        denom = sum(jnp.sum(t, axis=0, keepdims=True) for t in p)
        ot = sum(jnp.dot(vt_ring[b, rows, :], p[b].astype(bf16), preferred_element_type=f32)
                 for b in range(N_SLOTS))
        at_buf[rows, :] = ot / denom

    g_att = proj(3 * d_att, d_att)
    a = at_buf[...].T * _silu(g_att)

    c0 = 4 * d_att
    c_val = proj(c0, d_conv)
    c_glu = proj(c0 + d_conv, d_conv)
    hg_buf[CONV_HIST:CONV_HIST + TILE, :] = c_val * jax.nn.sigmoid(c_glu)
    first = CONV_HIST - (CONV_WIDTH - 1)
    conv_cols = []
    for c in range(d_conv // LANES):
        cols = slice(c * LANES, (c + 1) * LANES)
        acc = jnp.broadcast_to(dw_b_ref[:, cols], (TILE, LANES))
        for t in range(CONV_WIDTH):
            acc = acc + dw_w_ref[t:t + 1, cols] * hg_buf[first + t:first + t + TILE, cols]
        conv_cols.append(acc)
    hc = jnp.concatenate(conv_cols, axis=-1)
    mu = jnp.mean(hc, axis=-1, keepdims=True)
    var = jnp.mean(jnp.square(hc - mu), axis=-1, keepdims=True)
    hn = (hc - mu) * lax.rsqrt(var + EPS) * ln_g_ref[...] + ln_b_ref[...]
    hs = _silu(hn).astype(bf16)
    pw = jnp.dot(hs, w_pw_ref[...], preferred_element_type=f32) + b_pw_ref[...]
    g_conv = proj(c0 + 2 * d_conv, d_conv)
    cmod = pw * _silu(g_conv)

    y = (jnp.dot(a.astype(bf16), w_out_ref[0:d_att, :], preferred_element_type=f32)
         + jnp.dot(cmod.astype(bf16), w_out_ref[d_att:d_att + d_conv, :],
                   preferred_element_type=f32))
    out_ref[...] = x + _rmsnorm(y, post_g_ref[...])

    hg_buf[0:CONV_HIST, :] = hg_buf[TILE:TILE + CONV_HIST, :]


def _score_bias(rel_bias):
    rb = rel_bias.astype(jnp.float32)
    n_heads = rb.shape[0]
    width = BAND + TILE
    lo = BAND - 1 - HIST - MAX_REL
    by_dist = jnp.concatenate([
        jnp.broadcast_to(rb[:, :1], (n_heads, lo)),
        rb,
        jnp.broadcast_to(rb[:, -1:], (n_heads, width - lo - rb.shape[1])),
    ], axis=1)
    flat = jnp.broadcast_to(by_dist[:, None, :], (n_heads, BAND, width)).reshape(n_heads, -1)
    skew = flat[:, :BAND * (width - 1)].reshape(n_heads, BAND, width - 1)
    bias = skew[:, :, BAND - 1:BAND - 1 + TILE]
    r = np.arange(BAND)[:, None]
    c = np.arange(TILE)[None, :]
    key_chunk = r // CHUNK
    q_chunk = c // CHUNK
    in_band = (key_chunk >= q_chunk) & (key_chunk <= q_chunk + LEFT_CHUNKS)
    return jnp.where(jnp.asarray(in_band)[None], bias, jnp.float32(MASK_VALUE))


def _const_spec(shape):
    return pl.BlockSpec(shape, lambda b, s: (0,) * len(shape))


def _layer(x, pre_g, w_in, rel_bias, dw_w, dw_b, ln_g, ln_b, w_pw, b_pw, w_out, post_g):
    batch, seq, d_model = x.shape
    d_att = N_HEADS * HEAD_DIM
    d_conv = w_pw.shape[0]
    assert seq % TILE == 0 and d_att % LANES == 0 and d_conv % LANES == 0
    assert w_in.shape == (d_model, 4 * d_att + 3 * d_conv)
    assert w_out.shape == (d_att + d_conv, d_model)
    row = lambda p: p.reshape(1, -1).astype(jnp.float32)
    bf16 = MXU_DTYPE
    kernel = functools.partial(_layer_kernel, d_att=d_att, d_conv=d_conv)
    tile_spec = pl.BlockSpec((None, TILE, d_model), lambda b, s: (b, s, 0))
    return pl.pallas_call(
        kernel,
        grid=(batch, seq // TILE),
        in_specs=[
            tile_spec,
            _const_spec((1, d_model)),
            _const_spec(w_in.shape),
            _const_spec((N_HEADS, BAND, TILE)),
            _const_spec(dw_w.shape),
            _const_spec((1, d_conv)),
            _const_spec((1, d_conv)),
            _const_spec((1, d_conv)),
            _const_spec(w_pw.shape),
            _const_spec((1, d_conv)),
            _const_spec(w_out.shape),
            _const_spec((1, d_model)),
        ],
        out_specs=tile_spec,
        out_shape=jax.ShapeDtypeStruct(x.shape, x.dtype),
        scratch_shapes=[
            pltpu.VMEM((BAND, d_att), bf16),
            pltpu.VMEM((N_SLOTS, d_att, TILE), bf16),
            pltpu.VMEM((CONV_HIST + TILE, d_conv), jnp.float32),
            pltpu.VMEM((N_HEADS, BAND, TILE), jnp.float32),
            pltpu.VMEM((d_att, TILE), jnp.float32),
        ],
        compiler_params=pltpu.CompilerParams(
            dimension_semantics=("arbitrary", "arbitrary"),
            vmem_limit_bytes=VMEM_LIMIT_BYTES),
        name="hymba_layer",
    )(x, row(pre_g), w_in.astype(bf16), _score_bias(rel_bias), dw_w.astype(jnp.float32),
      row(dw_b), row(ln_g), row(ln_b), w_pw.astype(bf16), row(b_pw), w_out.astype(bf16),
      row(post_g))


def kernel(x, pre_norm_g, w_in, rel_bias, dw_w, dw_b, conv_ln_g, conv_ln_b, w_pw, b_pw,
           w_out, post_norm_g):
    for l in range(pre_norm_g.shape[0]):
        x = _layer(x, pre_norm_g[l], w_in[l], rel_bias[l], dw_w[l], dw_b[l], conv_ln_g[l],
                   conv_ln_b[l], w_pw[l], b_pw[l], w_out[l], post_norm_g[l])
    return x
```

```python
import functools

import jax
import jax.numpy as jnp
from jax import lax
from jax.experimental import pallas as pl
from jax.experimental.pallas import tpu as pltpu

CHUNK = 64
LEFT_CHUNKS = 8
N_HEADS = 8
HEAD_DIM = 64
MAX_REL = 128
CONV_WIDTH = 31
EPS = 1e-6
MASK_VALUE = -1e30
MXU_DTYPE = jnp.bfloat16

LANES = 128
SUBLANES = 8
VMEM_LIMIT_BYTES = 56 * 1024 * 1024

TILE = 4 * CHUNK
HIST = LEFT_CHUNKS * CHUNK
BAND = HIST + TILE
N_SLOTS = BAND // TILE
CONV_HIST = 32
HEADS_PER_SLAB = LANES // HEAD_DIM


def _rmsnorm(v, g):
    ms = jnp.mean(v * v, axis=-1, keepdims=True)
    return v * lax.rsqrt(ms + EPS) * g


def _silu(v):
    return v * jax.nn.sigmoid(v)


def _layer_kernel(x_ref, pre_g_ref, w_in_ref, bias_ref, dw_w_ref, dw_b_ref,
                  ln_g_ref, ln_b_ref, w_pw_ref, b_pw_ref, w_out_ref, post_g_ref,
                  out_ref,
                  k_ring, vt_ring, hg_buf, eff_ref, at_buf, *, d_att, d_conv):
    s = pl.program_id(1)
    f32 = jnp.float32
    bf16 = MXU_DTYPE

    cur = lax.rem(s, N_SLOTS)

    @pl.when(s == 0)
    def _():
        k_ring[...] = jnp.zeros(k_ring.shape, bf16)
        vt_ring[...] = jnp.zeros(vt_ring.shape, bf16)
        hg_buf[0:CONV_HIST, :] = jnp.zeros((CONV_HIST, d_conv), f32)

    @pl.when(s <= HIST // TILE)
    def _():
        key_pos = s * TILE - HIST + lax.broadcasted_iota(jnp.int32, (BAND, TILE), 0)
        start_mask = jnp.where(key_pos >= 0, 0.0, MASK_VALUE).astype(f32)
        for h in range(N_HEADS):
            eff_ref[h] = bias_ref[h] + start_mask

    x = x_ref[...]
    h_in = _rmsnorm(x, pre_g_ref[...]).astype(bf16)

    def proj(i, width):
        return jnp.dot(h_in, w_in_ref[:, i:i + width], preferred_element_type=f32)

    q = proj(0, d_att) * (HEAD_DIM ** -0.5)
    k = proj(d_att, d_att)
    v = proj(2 * d_att, d_att)
    k_ring[pl.ds(pl.multiple_of(cur * TILE, TILE), TILE), :] = k.astype(bf16)
    vt_ring[cur] = v.T.astype(bf16)

    lane = lax.broadcasted_iota(jnp.int32, (TILE, d_att), 1)
    head_in_slab = (lane // HEAD_DIM) % HEADS_PER_SLAB
    q_bf = q.astype(bf16)
    q_sel = [jnp.where(head_in_slab == j, q_bf, jnp.zeros_like(q_bf))
             for j in range(HEADS_PER_SLAB)]

    bias_row = [pl.multiple_of((N_SLOTS - 1 - lax.rem(cur - b + N_SLOTS, N_SLOTS)) * TILE, TILE)
                for b in range(N_SLOTS)]

    for h in range(N_HEADS):
        slab = slice((h // HEADS_PER_SLAB) * LANES, (h // HEADS_PER_SLAB + 1) * LANES)
        rows = slice(h * HEAD_DIM, (h + 1) * HEAD_DIM)
        st = lax.dot_general(k_ring[:, slab], q_sel[h % HEADS_PER_SLAB][:, slab],
                             (((1,), (1,)), ((), ())), preferred_element_type=f32)
        st = [st[b * TILE:(b + 1) * TILE] + eff_ref[h, pl.ds(bias_row[b], TILE), :]
              for b in range(N_SLOTS)]
        m = functools.reduce(jnp.maximum, [jnp.max(t, axis=0, keepdims=True) for t in st])
        p = [jnp.exp(t - m) for t in st]
        denom = sum(jnp.sum(t, axis=0, keepdims=True) for t in p)
        ot = sum(jnp.dot(vt_ring[b, rows, :], p[b].astype(bf16), preferred_element_type=f32)
                 for b in range(N_SLOTS))
        at_buf[rows, :] = ot / denom

    g_att = proj(3 * d_att, d_att)
    a = at_buf[...].T * _silu(g_att)

    c0 = 4 * d_att
    c_val = proj(c0, d_conv)
    c_glu = proj(c0 + d_conv, d_conv)
    hg_buf[CONV_HIST:CONV_HIST + TILE, :] = c_val * jax.nn.sigmoid(c_glu)
    first = CONV_HIST - (CONV_WIDTH - 1)
    conv_cols = []
    for c in range(d_conv // LANES):
        cols = slice(c * LANES, (c + 1) * LANES)
        acc = jnp.broadcast_to(dw_b_ref[:, cols], (TILE, LANES))
        for t in range(CONV_WIDTH):
            acc = acc + dw_w_ref[t:t + 1, cols] * hg_buf[first + t:first + t + TILE, cols]
        conv_cols.append(acc)
    hc = jnp.concatenate(conv_cols, axis=-1)
    mu = jnp.mean(hc, axis=-1, keepdims=True)
    var = jnp.mean(jnp.square(hc - mu), axis=-1, keepdims=True)
    hn = (hc - mu) * lax.rsqrt(var + EPS) * ln_g_ref[...] + ln_b_ref[...]
    hs = _silu(hn).astype(bf16)
    pw = jnp.dot(hs, w_pw_ref[...], preferred_element_type=f32) + b_pw_ref[...]
    g_conv = proj(c0 + 2 * d_conv, d_conv)
    cmod = pw * _silu(g_conv)

    y = (jnp.dot(a.astype(bf16), w_out_ref[0:d_att, :], preferred_element_type=f32)
         + jnp.dot(cmod.astype(bf16), w_out_ref[d_att:d_att + d_conv, :],
                   preferred_element_type=f32))
    out_ref[...] = x + _rmsnorm(y, post_g_ref[...])

    hg_buf[0:CONV_HIST, :] = hg_buf[TILE:TILE + CONV_HIST, :]


def _bias_kernel(shifted_ref, out_ref):
    q_chunk = lax.broadcasted_iota(jnp.int32, (SUBLANES, TILE), 1) // CHUNK
    for g in range(BAND // SUBLANES):
        off = BAND - 1 - SUBLANES * g
        key_chunk = (SUBLANES * g) // CHUNK
        in_band = (q_chunk <= key_chunk) & (q_chunk >= key_chunk - LEFT_CHUNKS)
        window = shifted_ref[:, off:off + TILE]
        out_ref[SUBLANES * g:SUBLANES * (g + 1), :] = jnp.where(in_band, window, MASK_VALUE)


def _score_bias(rel_bias):
    rb = rel_bias.astype(jnp.float32)
    n_heads = rb.shape[0]
    width = BAND + TILE
    lo = BAND - 1 - HIST - MAX_REL
    by_dist = jnp.concatenate([
        jnp.zeros((n_heads, SUBLANES - 1), jnp.float32),
        jnp.broadcast_to(rb[:, :1], (n_heads, lo)),
        rb,
        jnp.broadcast_to(rb[:, -1:], (n_heads, width - lo - rb.shape[1])),
    ], axis=1)
    shifted = jnp.stack([by_dist[:, SUBLANES - 1 - i:SUBLANES - 1 - i + width]
                         for i in range(SUBLANES)], axis=1)
    return pl.pallas_call(
        _bias_kernel,
        grid=(n_heads,),
        in_specs=[pl.BlockSpec((None, SUBLANES, width), lambda h: (h, 0, 0))],
        out_specs=pl.BlockSpec((None, BAND, TILE), lambda h: (h, 0, 0)),
        out_shape=jax.ShapeDtypeStruct((n_heads, BAND, TILE), jnp.float32),
        compiler_params=pltpu.CompilerParams(dimension_semantics=("arbitrary",)),
        name="score_bias_table",
    )(shifted)


def _const_spec(shape):
    return pl.BlockSpec(shape, lambda b, s: (0,) * len(shape))


def _layer(x, pre_g, w_in, rel_bias, dw_w, dw_b, ln_g, ln_b, w_pw, b_pw, w_out, post_g):
    batch, seq, d_model = x.shape
    d_att = N_HEADS * HEAD_DIM
    d_conv = w_pw.shape[0]
    assert seq % TILE == 0 and d_att % LANES == 0 and d_conv % LANES == 0
    assert w_in.shape == (d_model, 4 * d_att + 3 * d_conv)
    assert w_out.shape == (d_att + d_conv, d_model)
    row = lambda p: p.reshape(1, -1).astype(jnp.float32)
    bf16 = MXU_DTYPE
    kernel = functools.partial(_layer_kernel, d_att=d_att, d_conv=d_conv)
    tile_spec = pl.BlockSpec((None, TILE, d_model), lambda b, s: (b, s, 0))
    return pl.pallas_call(
        kernel,
        grid=(batch, seq // TILE),
        in_specs=[
            tile_spec,
            _const_spec((1, d_model)),
            _const_spec(w_in.shape),
            _const_spec((N_HEADS, BAND, TILE)),
            _const_spec(dw_w.shape),
            _const_spec((1, d_conv)),
            _const_spec((1, d_conv)),
            _const_spec((1, d_conv)),
            _const_spec(w_pw.shape),
            _const_spec((1, d_conv)),
            _const_spec(w_out.shape),
            _const_spec((1, d_model)),
        ],
        out_specs=tile_spec,
        out_shape=jax.ShapeDtypeStruct(x.shape, x.dtype),
        scratch_shapes=[
            pltpu.VMEM((BAND, d_att), bf16),
            pltpu.VMEM((N_SLOTS, d_att, TILE), bf16),
            pltpu.VMEM((CONV_HIST + TILE, d_conv), jnp.float32),
            pltpu.VMEM((N_HEADS, BAND, TILE), jnp.float32),
            pltpu.VMEM((d_att, TILE), jnp.float32),
        ],
        compiler_params=pltpu.CompilerParams(
            dimension_semantics=("arbitrary", "arbitrary"),
            vmem_limit_bytes=VMEM_LIMIT_BYTES),
        name="hymba_layer",
    )(x, row(pre_g), w_in.astype(bf16), _score_bias(rel_bias), dw_w.astype(jnp.float32),
      row(dw_b), row(ln_g), row(ln_b), w_pw.astype(bf16), row(b_pw), w_out.astype(bf16),
      row(post_g))


def kernel(x, pre_norm_g, w_in, rel_bias, dw_w, dw_b, conv_ln_g, conv_ln_b, w_pw, b_pw,
           w_out, post_norm_g):
    for l in range(pre_norm_g.shape[0]):
        x = _layer(x, pre_norm_g[l], w_in[l], rel_bias[l], dw_w[l], dw_b[l], conv_ln_g[l],
                   conv_ln_b[l], w_pw[l], b_pw[l], w_out[l], post_norm_g[l])
    return x
```

```python
import functools

import jax
import jax.numpy as jnp
from jax import lax
from jax.experimental import pallas as pl
from jax.experimental.pallas import tpu as pltpu

CHUNK = 64
LEFT_CHUNKS = 8
N_HEADS = 8
HEAD_DIM = 64
MAX_REL = 128
CONV_WIDTH = 31
EPS = 1e-6
MASK_VALUE = -1e30
LOG2_E = 1.4426950408889634
MXU_DTYPE = jnp.bfloat16

LANES = 128
SUBLANES = 8
VMEM_LIMIT_BYTES = 56 * 1024 * 1024

TILE = 4 * CHUNK
HIST = LEFT_CHUNKS * CHUNK
BAND = HIST + TILE
N_SLOTS = BAND // TILE
CONV_HIST = 32
HEADS_PER_SLAB = LANES // HEAD_DIM
CONV_ROW_SPLIT = 2


def _rmsnorm(v, g):
    ms = jnp.mean(v * v, axis=-1, keepdims=True)
    return v * lax.rsqrt(ms + EPS) * g


def _silu(v):
    return v * jax.nn.sigmoid(v)


def _layer_kernel(x_ref, pre_g_ref, w_in_ref, bias_ref, dw_w_ref, dw_b_ref,
                  ln_g_ref, ln_b_ref, w_pw_ref, b_pw_ref, w_out_ref, post_g_ref,
                  out_ref,
                  k_ring, vt_ring, hg_buf, eff_ref, at_buf, hc_buf, *, d_att, d_conv):
    s = pl.program_id(1)
    f32 = jnp.float32
    bf16 = MXU_DTYPE

    cur = lax.rem(s, N_SLOTS)

    @pl.when(s == 0)
    def _():
        k_ring[...] = jnp.zeros(k_ring.shape, bf16)
        vt_ring[...] = jnp.zeros(vt_ring.shape, bf16)
        hg_buf[0:CONV_HIST, :] = jnp.zeros((CONV_HIST, d_conv), f32)

    @pl.when(s <= HIST // TILE)
    def _():
        key_pos = s * TILE - HIST + lax.broadcasted_iota(jnp.int32, (BAND, TILE), 0)
        start_mask = jnp.where(key_pos >= 0, 0.0, MASK_VALUE).astype(f32)
        for h in range(N_HEADS):
            eff_ref[h] = bias_ref[h] + start_mask

    x = x_ref[...]
    h_in = _rmsnorm(x, pre_g_ref[...]).astype(bf16)

    def proj(i, width):
        return jnp.dot(h_in, w_in_ref[:, i:i + width], preferred_element_type=f32)

    c0 = 4 * d_att
    c_val = proj(c0, d_conv)
    c_glu = proj(c0 + d_conv, d_conv)
    hg_buf[CONV_HIST:CONV_HIST + TILE, :] = c_val * jax.nn.sigmoid(c_glu)

    q = proj(0, d_att) * (HEAD_DIM ** -0.5 * LOG2_E)
    k = proj(d_att, d_att)
    v = proj(2 * d_att, d_att)
    k_ring[pl.ds(pl.multiple_of(cur * TILE, TILE), TILE), :] = k.astype(bf16)
    vt_ring[cur] = v.T.astype(bf16)

    lane = lax.broadcasted_iota(jnp.int32, (TILE, d_att), 1)
    head_in_slab = (lane // HEAD_DIM) % HEADS_PER_SLAB
    q_bf = q.astype(bf16)
    q_sel = [jnp.where(head_in_slab == j, q_bf, jnp.zeros_like(q_bf))
             for j in range(HEADS_PER_SLAB)]

    bias_row = [pl.multiple_of((N_SLOTS - 1 - lax.rem(cur - b + N_SLOTS, N_SLOTS)) * TILE, TILE)
                for b in range(N_SLOTS)]

    def scores_t(h):
        slab = slice((h // HEADS_PER_SLAB) * LANES, (h // HEADS_PER_SLAB + 1) * LANES)
        return lax.dot_general(k_ring[:, slab], q_sel[h % HEADS_PER_SLAB][:, slab],
                               (((1,), (1,)), ((), ())), preferred_element_type=f32)

    def conv_piece(piece):
        c, half = divmod(piece, CONV_ROW_SPLIT)
        rows = TILE // CONV_ROW_SPLIT
        i0 = half * rows
        cols = slice(c * LANES, (c + 1) * LANES)
        first = CONV_HIST - (CONV_WIDTH - 1)
        acc = jnp.broadcast_to(dw_b_ref[:, cols], (rows, LANES))
        for r in range(SUBLANES):
            n = rows + (SUBLANES if r else 0)
            part = None
            for t in range(CONV_WIDTH):
                if (first + t) % SUBLANES != r:
                    continue
                base = first + t - r + i0
                term = dw_w_ref[t:t + 1, cols] * hg_buf[base:base + n, cols]
                part = term if part is None else part + term
            acc = acc + part[r:r + rows]
        hc_buf[i0:i0 + rows, cols] = acc

    n_conv_pieces = (d_conv // LANES) * CONV_ROW_SPLIT
    st_next = scores_t(0)
    for h in range(N_HEADS):
        rows = slice(h * HEAD_DIM, (h + 1) * HEAD_DIM)
        st = st_next
        if h + 1 < N_HEADS:
            st_next = scores_t(h + 1)
        st = [st[b * TILE:(b + 1) * TILE] + eff_ref[h, pl.ds(bias_row[b], TILE), :]
              for b in range(N_SLOTS)]
        m = functools.reduce(jnp.maximum, [jnp.max(t, axis=0, keepdims=True) for t in st])
        p = [jnp.exp2(t - m) for t in st]
        denom = sum(jnp.sum(t, axis=0, keepdims=True) for t in p)
        for piece in range(h * n_conv_pieces // N_HEADS, (h + 1) * n_conv_pieces // N_HEADS):
            conv_piece(piece)
        ot = sum(jnp.dot(vt_ring[b, rows, :], p[b].astype(bf16), preferred_element_type=f32)
                 for b in range(N_SLOTS))
        at_buf[rows, :] = ot / denom

    g_att = proj(3 * d_att, d_att)
    a = at_buf[...].T * _silu(g_att)

    hc = hc_buf[...]
    mu = jnp.mean(hc, axis=-1, keepdims=True)
    var = jnp.mean(jnp.square(hc - mu), axis=-1, keepdims=True)
    hn = (hc - mu) * lax.rsqrt(var + EPS) * ln_g_ref[...] + ln_b_ref[...]
    hs = _silu(hn).astype(bf16)
    pw = jnp.dot(hs, w_pw_ref[...], preferred_element_type=f32) + b_pw_ref[...]
    g_conv = proj(c0 + 2 * d_conv, d_conv)
    cmod = pw * _silu(g_conv)

    y = (jnp.dot(a.astype(bf16), w_out_ref[0:d_att, :], preferred_element_type=f32)
         + jnp.dot(cmod.astype(bf16), w_out_ref[d_att:d_att + d_conv, :],
                   preferred_element_type=f32))
    out_ref[...] = x + _rmsnorm(y, post_g_ref[...])

    hg_buf[0:CONV_HIST, :] = hg_buf[TILE:TILE + CONV_HIST, :]


def _bias_kernel(shifted_ref, out_ref):
    q_chunk = lax.broadcasted_iota(jnp.int32, (SUBLANES, TILE), 1) // CHUNK
    for g in range(BAND // SUBLANES):
        off = BAND - 1 - SUBLANES * g
        key_chunk = (SUBLANES * g) // CHUNK
        in_band = (q_chunk <= key_chunk) & (q_chunk >= key_chunk - LEFT_CHUNKS)
        window = shifted_ref[:, off:off + TILE]
        out_ref[SUBLANES * g:SUBLANES * (g + 1), :] = jnp.where(in_band, window * LOG2_E, MASK_VALUE)


def _score_bias(rel_bias):
    rb = rel_bias.astype(jnp.float32)
    n_heads = rb.shape[0]
    width = BAND + TILE
    lo = BAND - 1 - HIST - MAX_REL
    by_dist = jnp.concatenate([
        jnp.zeros((n_heads, SUBLANES - 1), jnp.float32),
        jnp.broadcast_to(rb[:, :1], (n_heads, lo)),
        rb,
        jnp.broadcast_to(rb[:, -1:], (n_heads, width - lo - rb.shape[1])),
    ], axis=1)
    shifted = jnp.stack([by_dist[:, SUBLANES - 1 - i:SUBLANES - 1 - i + width]
                         for i in range(SUBLANES)], axis=1)
    return pl.pallas_call(
        _bias_kernel,
        grid=(n_heads,),
        in_specs=[pl.BlockSpec((None, SUBLANES, width), lambda h: (h, 0, 0))],
        out_specs=pl.BlockSpec((None, BAND, TILE), lambda h: (h, 0, 0)),
        out_shape=jax.ShapeDtypeStruct((n_heads, BAND, TILE), jnp.float32),
        compiler_params=pltpu.CompilerParams(dimension_semantics=("arbitrary",)),
        name="score_bias_table",
    )(shifted)


def _const_spec(shape):
    return pl.BlockSpec(shape, lambda b, s: (0,) * len(shape))


def _layer(x, pre_g, w_in, rel_bias, dw_w, dw_b, ln_g, ln_b, w_pw, b_pw, w_out, post_g):
    batch, seq, d_model = x.shape
    d_att = N_HEADS * HEAD_DIM
    d_conv = w_pw.shape[0]
    assert seq % TILE == 0 and d_att % LANES == 0 and d_conv % LANES == 0
    assert w_in.shape == (d_model, 4 * d_att + 3 * d_conv)
    assert w_out.shape == (d_att + d_conv, d_model)
    row = lambda p: p.reshape(1, -1).astype(jnp.float32)
    bf16 = MXU_DTYPE
    kernel = functools.partial(_layer_kernel, d_att=d_att, d_conv=d_conv)
    tile_spec = pl.BlockSpec((None, TILE, d_model), lambda b, s: (b, s, 0))
    return pl.pallas_call(
        kernel,
        grid=(batch, seq // TILE),
        in_specs=[
            tile_spec,
            _const_spec((1, d_model)),
            _const_spec(w_in.shape),
            _const_spec((N_HEADS, BAND, TILE)),
            _const_spec(dw_w.shape),
            _const_spec((1, d_conv)),
            _const_spec((1, d_conv)),
            _const_spec((1, d_conv)),
            _const_spec(w_pw.shape),
            _const_spec((1, d_conv)),
            _const_spec(w_out.shape),
            _const_spec((1, d_model)),
        ],
        out_specs=tile_spec,
        out_shape=jax.ShapeDtypeStruct(x.shape, x.dtype),
        scratch_shapes=[
            pltpu.VMEM((BAND, d_att), bf16),
            pltpu.VMEM((N_SLOTS, d_att, TILE), bf16),
            pltpu.VMEM((CONV_HIST + TILE, d_conv), jnp.float32),
            pltpu.VMEM((N_HEADS, BAND, TILE), jnp.float32),
            pltpu.VMEM((d_att, TILE), jnp.float32),
            pltpu.VMEM((TILE, d_conv), jnp.float32),
        ],
        compiler_params=pltpu.CompilerParams(
            dimension_semantics=("arbitrary", "arbitrary"),
            vmem_limit_bytes=VMEM_LIMIT_BYTES),
        name="hymba_layer",
    )(x, row(pre_g), w_in.astype(bf16), _score_bias(rel_bias), dw_w.astype(jnp.float32),
      row(dw_b), row(ln_g), row(ln_b), w_pw.astype(bf16), row(b_pw), w_out.astype(bf16),
      row(post_g))


def kernel(x, pre_norm_g, w_in, rel_bias, dw_w, dw_b, conv_ln_g, conv_ln_b, w_pw, b_pw,
           w_out, post_norm_g):
    for l in range(pre_norm_g.shape[0]):
        x = _layer(x, pre_norm_g[l], w_in[l], rel_bias[l], dw_w[l], dw_b[l], conv_ln_g[l],
                   conv_ln_b[l], w_pw[l], b_pw[l], w_out[l], post_norm_g[l])
    return x
```

```python
import functools

import jax
import jax.numpy as jnp
from jax import lax
from jax.experimental import pallas as pl
from jax.experimental.pallas import tpu as pltpu

CHUNK = 64
LEFT_CHUNKS = 8
N_HEADS = 8
HEAD_DIM = 64
MAX_REL = 128
CONV_WIDTH = 31
EPS = 1e-6
MASK_VALUE = -1e30
LOG2_E = 1.4426950408889634
MXU_DTYPE = jnp.bfloat16

LANES = 128
SUBLANES = 8
BF16_SUBLANES = 16
VMEM_LIMIT_BYTES = 56 * 1024 * 1024

TILE = 4 * CHUNK
HIST = LEFT_CHUNKS * CHUNK
BAND = HIST + TILE
N_SLOTS = BAND // TILE
CONV_HIST = 32
HEADS_PER_SLAB = LANES // HEAD_DIM
CONV_ROW_SPLIT = 2
CONV_COL_SPLIT = 2


def _rmsnorm(v, g):
    ms = jnp.mean(v * v, axis=-1, keepdims=True)
    return v * lax.rsqrt(ms + EPS) * g


def _silu(v):
    return v * jax.nn.sigmoid(v)


def _layer_kernel(x_ref, pre_g_ref, w_in_ref, bias_ref, dw_w_ref, dw_b_ref,
                  ln_g_ref, ln_b_ref, w_pw_ref, b_pw_ref, w_out_ref, post_g_ref,
                  out_ref,
                  k_ring, vt_ring, hg_buf, eff_ref, at_buf, hc_buf, *, d_att, d_conv):
    s = pl.program_id(1)
    f32 = jnp.float32
    bf16 = MXU_DTYPE

    cur = lax.rem(s, N_SLOTS)

    @pl.when(s == 0)
    def _():
        k_ring[...] = jnp.zeros(k_ring.shape, bf16)
        vt_ring[...] = jnp.zeros(vt_ring.shape, bf16)
        hg_buf[0:CONV_HIST, :] = jnp.zeros((CONV_HIST, d_conv), f32)

    @pl.when(s <= HIST // TILE)
    def _():
        key_pos = s * TILE - HIST + lax.broadcasted_iota(jnp.int32, (BAND, TILE), 0)
        start_mask = jnp.where(key_pos >= 0, 0.0, MASK_VALUE).astype(f32)
        for h in range(N_HEADS):
            eff_ref[h] = bias_ref[h] + start_mask

    x = x_ref[...]
    h_in = _rmsnorm(x, pre_g_ref[...]).astype(bf16)

    def proj(i, width):
        return jnp.dot(h_in, w_in_ref[:, i:i + width], preferred_element_type=f32)

    c0 = 4 * d_att

    def glu_part(part):
        width = d_conv // CONV_COL_SPLIT
        lo = part * width
        c_val = proj(c0 + lo, width)
        c_glu = proj(c0 + d_conv + lo, width)
        hg_buf[CONV_HIST:CONV_HIST + TILE, lo:lo + width] = c_val * jax.nn.sigmoid(c_glu)

    def conv_piece(piece):
        c, half = divmod(piece, CONV_ROW_SPLIT)
        rows = TILE // CONV_ROW_SPLIT
        i0 = half * rows
        cols = slice(c * LANES, (c + 1) * LANES)
        first = CONV_HIST - (CONV_WIDTH - 1)
        acc = jnp.broadcast_to(dw_b_ref[:, cols], (rows, LANES))
        for r in range(SUBLANES):
            n = rows + (SUBLANES if r else 0)
            part = None
            for t in range(CONV_WIDTH):
                if (first + t) % SUBLANES != r:
                    continue
                base = first + t - r + i0
                term = dw_w_ref[t:t + 1, cols] * hg_buf[base:base + n, cols]
                part = term if part is None else part + term
            acc = acc + part[r:r + rows]
        hc_buf[i0:i0 + rows, cols] = acc

    pieces_per_part = (d_conv // LANES) * CONV_ROW_SPLIT // CONV_COL_SPLIT
    for part in range(CONV_COL_SPLIT):
        glu_part(part)
        for piece in range(part * pieces_per_part, (part + 1) * pieces_per_part):
            conv_piece(piece)

    q = proj(0, d_att) * (HEAD_DIM ** -0.5 * LOG2_E)
    k = proj(d_att, d_att)
    v = proj(2 * d_att, d_att)
    k_ring[pl.ds(pl.multiple_of(cur * TILE, TILE), TILE), :] = k.astype(bf16)
    vt_ring[cur] = v.T.astype(bf16)
    gate_att = _silu(proj(3 * d_att, d_att))
    gate_conv = _silu(proj(c0 + 2 * d_conv, d_conv))

    lane = lax.broadcasted_iota(jnp.int32, (TILE, d_att), 1)
    head_in_slab = (lane // HEAD_DIM) % HEADS_PER_SLAB
    q_bf = q.astype(bf16)
    q_sel = [jnp.where(head_in_slab == j, q_bf, jnp.zeros_like(q_bf))
             for j in range(HEADS_PER_SLAB)]

    bias_row = [pl.multiple_of((N_SLOTS - 1 - lax.rem(cur - b + N_SLOTS, N_SLOTS)) * TILE, TILE)
                for b in range(N_SLOTS)]

    def scores_t(h):
        slab = slice((h // HEADS_PER_SLAB) * LANES, (h // HEADS_PER_SLAB + 1) * LANES)
        return lax.dot_general(k_ring[:, slab], q_sel[h % HEADS_PER_SLAB][:, slab],
                               (((1,), (1,)), ((), ())), preferred_element_type=f32)

    ones_rows = jnp.ones((BF16_SUBLANES, TILE), bf16)
    st_next = scores_t(0)
    for h in range(N_HEADS):
        rows = slice(h * HEAD_DIM, (h + 1) * HEAD_DIM)
        st = st_next
        if h + 1 < N_HEADS:
            st_next = scores_t(h + 1)
        st = [st[b * TILE:(b + 1) * TILE] + eff_ref[h, pl.ds(bias_row[b], TILE), :]
              for b in range(N_SLOTS)]
        m = functools.reduce(jnp.maximum, [jnp.max(t, axis=0, keepdims=True) for t in st])
        p = [jnp.exp2(t - m).astype(bf16) for t in st]
        ot = sum(jnp.dot(jnp.concatenate([vt_ring[b, rows, :], ones_rows], axis=0), p[b],
                         preferred_element_type=f32) for b in range(N_SLOTS))
        at_buf[rows, :] = ot[:HEAD_DIM] / ot[HEAD_DIM:HEAD_DIM + 1]

    a = at_buf[...].T * gate_att

    hc = hc_buf[...]
    mu = jnp.mean(hc, axis=-1, keepdims=True)
    var = jnp.mean(jnp.square(hc - mu), axis=-1, keepdims=True)
    hn = (hc - mu) * lax.rsqrt(var + EPS) * ln_g_ref[...] + ln_b_ref[...]
    hs = _silu(hn).astype(bf16)
    pw = jnp.dot(hs, w_pw_ref[...], preferred_element_type=f32) + b_pw_ref[...]
    cmod = pw * gate_conv

    y = (jnp.dot(a.astype(bf16), w_out_ref[0:d_att, :], preferred_element_type=f32)
         + jnp.dot(cmod.astype(bf16), w_out_ref[d_att:d_att + d_conv, :],
                   preferred_element_type=f32))
    out_ref[...] = x + _rmsnorm(y, post_g_ref[...])

    hg_buf[0:CONV_HIST, :] = hg_buf[TILE:TILE + CONV_HIST, :]


def _bias_kernel(shifted_ref, out_ref):
    q_chunk = lax.broadcasted_iota(jnp.int32, (SUBLANES, TILE), 1) // CHUNK
    for g in range(BAND // SUBLANES):
        off = BAND - 1 - SUBLANES * g
        key_chunk = (SUBLANES * g) // CHUNK
        in_band = (q_chunk <= key_chunk) & (q_chunk >= key_chunk - LEFT_CHUNKS)
        window = shifted_ref[:, off:off + TILE]
        out_ref[SUBLANES * g:SUBLANES * (g + 1), :] = jnp.where(in_band, window * LOG2_E, MASK_VALUE)


def _score_bias(rel_bias):
    rb = rel_bias.astype(jnp.float32)
    n_heads = rb.shape[0]
    width = BAND + TILE
    lo = BAND - 1 - HIST - MAX_REL
    by_dist = jnp.concatenate([
        jnp.zeros((n_heads, SUBLANES - 1), jnp.float32),
        jnp.broadcast_to(rb[:, :1], (n_heads, lo)),
        rb,
        jnp.broadcast_to(rb[:, -1:], (n_heads, width - lo - rb.shape[1])),
    ], axis=1)
    shifted = jnp.stack([by_dist[:, SUBLANES - 1 - i:SUBLANES - 1 - i + width]
                         for i in range(SUBLANES)], axis=1)
    return pl.pallas_call(
        _bias_kernel,
        grid=(n_heads,),
        in_specs=[pl.BlockSpec((None, SUBLANES, width), lambda h: (h, 0, 0))],
        out_specs=pl.BlockSpec((None, BAND, TILE), lambda h: (h, 0, 0)),
        out_shape=jax.ShapeDtypeStruct((n_heads, BAND, TILE), jnp.float32),
        compiler_params=pltpu.CompilerParams(dimension_semantics=("arbitrary",)),
        name="score_bias_table",
    )(shifted)


def _const_spec(shape):
    return pl.BlockSpec(shape, lambda b, s: (0,) * len(shape))


def _layer(x, pre_g, w_in, rel_bias, dw_w, dw_b, ln_g, ln_b, w_pw, b_pw, w_out, post_g):
    batch, seq, d_model = x.shape
    d_att = N_HEADS * HEAD_DIM
    d_conv = w_pw.shape[0]
    assert seq % TILE == 0 and d_att % LANES == 0 and d_conv % LANES == 0
    assert w_in.shape == (d_model, 4 * d_att + 3 * d_conv)
    assert w_out.shape == (d_att + d_conv, d_model)
    row = lambda p: p.reshape(1, -1).astype(jnp.float32)
    bf16 = MXU_DTYPE
    kernel = functools.partial(_layer_kernel, d_att=d_att, d_conv=d_conv)
    tile_spec = pl.BlockSpec((None, TILE, d_model), lambda b, s: (b, s, 0))
    return pl.pallas_call(
        kernel,
        grid=(batch, seq // TILE),
        in_specs=[
            tile_spec,
            _const_spec((1, d_model)),
            _const_spec(w_in.shape),
            _const_spec((N_HEADS, BAND, TILE)),
            _const_spec(dw_w.shape),
            _const_spec((1, d_conv)),
            _const_spec((1, d_conv)),
            _const_spec((1, d_conv)),
            _const_spec(w_pw.shape),
            _const_spec((1, d_conv)),
            _const_spec(w_out.shape),
            _const_spec((1, d_model)),
        ],
        out_specs=tile_spec,
        out_shape=jax.ShapeDtypeStruct(x.shape, x.dtype),
        scratch_shapes=[
            pltpu.VMEM((BAND, d_att), bf16),
            pltpu.VMEM((N_SLOTS, d_att, TILE), bf16),
            pltpu.VMEM((CONV_HIST + TILE, d_conv), jnp.float32),
            pltpu.VMEM((N_HEADS, BAND, TILE), jnp.float32),
            pltpu.VMEM((d_att, TILE), jnp.float32),
            pltpu.VMEM((TILE, d_conv), jnp.float32),
        ],
        compiler_params=pltpu.CompilerParams(
            dimension_semantics=("arbitrary", "arbitrary"),
            vmem_limit_bytes=VMEM_LIMIT_BYTES),
        name="hymba_layer",
    )(x, row(pre_g), w_in.astype(bf16), _score_bias(rel_bias), dw_w.astype(jnp.float32),
      row(dw_b), row(ln_g), row(ln_b), w_pw.astype(bf16), row(b_pw), w_out.astype(bf16),
      row(post_g))


def kernel(x, pre_norm_g, w_in, rel_bias, dw_w, dw_b, conv_ln_g, conv_ln_b, w_pw, b_pw,
           w_out, post_norm_g):
    for l in range(pre_norm_g.shape[0]):
        x = _layer(x, pre_norm_g[l], w_in[l], rel_bias[l], dw_w[l], dw_b[l], conv_ln_g[l],
                   conv_ln_b[l], w_pw[l], b_pw[l], w_out[l], post_norm_g[l])
    return x
```

```python
import functools

import jax
import jax.numpy as jnp
from jax import lax
from jax.experimental import pallas as pl
from jax.experimental.pallas import tpu as pltpu

CHUNK = 64
LEFT_CHUNKS = 8
N_HEADS = 8
HEAD_DIM = 64
MAX_REL = 128
CONV_WIDTH = 31
EPS = 1e-6
MASK_VALUE = -1e30
LOG2_E = 1.4426950408889634
MXU_DTYPE = jnp.bfloat16

LANES = 128
SUBLANES = 8
BF16_SUBLANES = 16
VMEM_LIMIT_BYTES = 56 * 1024 * 1024

TILE = 4 * CHUNK
HIST = LEFT_CHUNKS * CHUNK
BAND = HIST + TILE
N_SLOTS = BAND // TILE
CONV_HIST = 32
HEADS_PER_SLAB = LANES // HEAD_DIM
CONV_ROW_SPLIT = 2
CONV_COL_SPLIT = 2
SCORE_LOOKAHEAD = 2


def _rmsnorm(v, g):
    ms = jnp.mean(v * v, axis=-1, keepdims=True)
    return v * lax.rsqrt(ms + EPS) * g


def _silu(v):
    return v * jax.nn.sigmoid(v)


def _layer_kernel(x_ref, pre_g_ref, w_in_ref, bias_ref, dw_w_ref, dw_b_ref,
                  ln_g_ref, ln_b_ref, w_pw_ref, b_pw_ref, w_out_ref, post_g_ref,
                  out_ref,
                  k_ring, vt_ring, hg_buf, eff_ref, at_buf, hc_buf, *, d_att, d_conv):
    s = pl.program_id(1)
    f32 = jnp.float32
    bf16 = MXU_DTYPE

    cur = lax.rem(s, N_SLOTS)

    @pl.when(s == 0)
    def _():
        k_ring[...] = jnp.zeros(k_ring.shape, bf16)
        vt_ring[...] = jnp.zeros(vt_ring.shape, bf16)
        hg_buf[0:CONV_HIST, :] = jnp.zeros((CONV_HIST, d_conv), f32)

    @pl.when(s <= HIST // TILE)
    def _():
        key_pos = s * TILE - HIST + lax.broadcasted_iota(jnp.int32, (BAND, TILE), 0)
        start_mask = jnp.where(key_pos >= 0, 0.0, MASK_VALUE).astype(f32)
        for h in range(N_HEADS):
            eff_ref[h] = bias_ref[h] + start_mask

    x = x_ref[...]
    h_in = _rmsnorm(x, pre_g_ref[...]).astype(bf16)

    def proj(i, width):
        return jnp.dot(h_in, w_in_ref[:, i:i + width], preferred_element_type=f32)

    c0 = 4 * d_att

    def glu_part(part):
        width = d_conv // CONV_COL_SPLIT
        lo = part * width
        c_val = proj(c0 + lo, width)
        c_glu = proj(c0 + d_conv + lo, width)
        hg_buf[CONV_HIST:CONV_HIST + TILE, lo:lo + width] = c_val * jax.nn.sigmoid(c_glu)

    def conv_piece(piece):
        c, half = divmod(piece, CONV_ROW_SPLIT)
        rows = TILE // CONV_ROW_SPLIT
        i0 = half * rows
        cols = slice(c * LANES, (c + 1) * LANES)
        first = CONV_HIST - (CONV_WIDTH - 1)
        acc = jnp.broadcast_to(dw_b_ref[:, cols], (rows, LANES))
        for r in range(SUBLANES):
            n = rows + (SUBLANES if r else 0)
            part = None
            for t in range(CONV_WIDTH):
                if (first + t) % SUBLANES != r:
                    continue
                base = first + t - r + i0
                term = dw_w_ref[t:t + 1, cols] * hg_buf[base:base + n, cols]
                part = term if part is None else part + term
            acc = acc + part[r:r + rows]
        hc_buf[i0:i0 + rows, cols] = acc

    pieces_per_part = (d_conv // LANES) * CONV_ROW_SPLIT // CONV_COL_SPLIT
    for part in range(CONV_COL_SPLIT):
        glu_part(part)
        for piece in range(part * pieces_per_part, (part + 1) * pieces_per_part):
            conv_piece(piece)

    q = proj(0, d_att) * (HEAD_DIM ** -0.5 * LOG2_E)
    k = proj(d_att, d_att)
    v = proj(2 * d_att, d_att)
    k_ring[pl.ds(pl.multiple_of(cur * TILE, TILE), TILE), :] = k.astype(bf16)
    vt_ring[cur] = v.T.astype(bf16)
    gate_att = _silu(proj(3 * d_att, d_att))
    gate_conv = _silu(proj(c0 + 2 * d_conv, d_conv))

    lane = lax.broadcasted_iota(jnp.int32, (TILE, d_att), 1)
    head_in_slab = (lane // HEAD_DIM) % HEADS_PER_SLAB
    q_bf = q.astype(bf16)
    q_sel = [jnp.where(head_in_slab == j, q_bf, jnp.zeros_like(q_bf))
             for j in range(HEADS_PER_SLAB)]

    bias_row = [pl.multiple_of((N_SLOTS - 1 - lax.rem(cur - b + N_SLOTS, N_SLOTS)) * TILE, TILE)
                for b in range(N_SLOTS)]

    def scores_t(h):
        slab = slice((h // HEADS_PER_SLAB) * LANES, (h // HEADS_PER_SLAB + 1) * LANES)
        return lax.dot_general(k_ring[:, slab], q_sel[h % HEADS_PER_SLAB][:, slab],
                               (((1,), (1,)), ((), ())), preferred_element_type=f32)

    ones_rows = jnp.ones((BF16_SUBLANES, TILE), bf16)
    pending = [scores_t(h) for h in range(SCORE_LOOKAHEAD)]
    for h in range(N_HEADS):
        rows = slice(h * HEAD_DIM, (h + 1) * HEAD_DIM)
        st = pending.pop(0)
        if h + SCORE_LOOKAHEAD < N_HEADS:
            pending.append(scores_t(h + SCORE_LOOKAHEAD))
        st = [st[b * TILE:(b + 1) * TILE] + eff_ref[h, pl.ds(bias_row[b], TILE), :]
              for b in range(N_SLOTS)]
        m = functools.reduce(jnp.maximum, [jnp.max(t, axis=0, keepdims=True) for t in st])
        p = [jnp.exp2(t - m).astype(bf16) for t in st]
        ot = sum(jnp.dot(jnp.concatenate([vt_ring[b, rows, :], ones_rows], axis=0), p[b],
                         preferred_element_type=f32) for b in range(N_SLOTS))
        at_buf[rows, :] = ot[:HEAD_DIM] / ot[HEAD_DIM:HEAD_DIM + 1]

    a = at_buf[...].T * gate_att

    hc = hc_buf[...]
    mu = jnp.mean(hc, axis=-1, keepdims=True)
    var = jnp.mean(jnp.square(hc - mu), axis=-1, keepdims=True)
    hn = (hc - mu) * lax.rsqrt(var + EPS) * ln_g_ref[...] + ln_b_ref[...]
    hs = _silu(hn).astype(bf16)
    pw = jnp.dot(hs, w_pw_ref[...], preferred_element_type=f32) + b_pw_ref[...]
    cmod = pw * gate_conv

    y = (jnp.dot(a.astype(bf16), w_out_ref[0:d_att, :], preferred_element_type=f32)
         + jnp.dot(cmod.astype(bf16), w_out_ref[d_att:d_att + d_conv, :],
                   preferred_element_type=f32))
    out_ref[...] = x + _rmsnorm(y, post_g_ref[...])

    hg_buf[0:CONV_HIST, :] = hg_buf[TILE:TILE + CONV_HIST, :]


def _bias_kernel(shifted_ref, out_ref):
    q_chunk = lax.broadcasted_iota(jnp.int32, (SUBLANES, TILE), 1) // CHUNK
    for g in range(BAND // SUBLANES):
        off = BAND - 1 - SUBLANES * g
        key_chunk = (SUBLANES * g) // CHUNK
        in_band = (q_chunk <= key_chunk) & (q_chunk >= key_chunk - LEFT_CHUNKS)
        window = shifted_ref[:, off:off + TILE]
        out_ref[SUBLANES * g:SUBLANES * (g + 1), :] = jnp.where(in_band, window * LOG2_E, MASK_VALUE)


def _score_bias(rel_bias):
    rb = rel_bias.astype(jnp.float32)
    n_heads = rb.shape[0]
    width = BAND + TILE
    lo = BAND - 1 - HIST - MAX_REL
    by_dist = jnp.concatenate([
        jnp.zeros((n_heads, SUBLANES - 1), jnp.float32),
        jnp.broadcast_to(rb[:, :1], (n_heads, lo)),
        rb,
        jnp.broadcast_to(rb[:, -1:], (n_heads, width - lo - rb.shape[1])),
    ], axis=1)
    shifted = jnp.stack([by_dist[:, SUBLANES - 1 - i:SUBLANES - 1 - i + width]
                         for i in range(SUBLANES)], axis=1)
    return pl.pallas_call(
        _bias_kernel,
        grid=(n_heads,),
        in_specs=[pl.BlockSpec((None, SUBLANES, width), lambda h: (h, 0, 0))],
        out_specs=pl.BlockSpec((None, BAND, TILE), lambda h: (h, 0, 0)),
        out_shape=jax.ShapeDtypeStruct((n_heads, BAND, TILE), jnp.float32),
        compiler_params=pltpu.CompilerParams(dimension_semantics=("arbitrary",)),
        name="score_bias_table",
    )(shifted)


def _const_spec(shape):
    return pl.BlockSpec(shape, lambda b, s: (0,) * len(shape))


def _layer(x, pre_g, w_in, rel_bias, dw_w, dw_b, ln_g, ln_b, w_pw, b_pw, w_out, post_g):
    batch, seq, d_model = x.shape
    d_att = N_HEADS * HEAD_DIM
    d_conv = w_pw.shape[0]
    assert seq % TILE == 0 and d_att % LANES == 0 and d_conv % LANES == 0
    assert w_in.shape == (d_model, 4 * d_att + 3 * d_conv)
    assert w_out.shape == (d_att + d_conv, d_model)
    row = lambda p: p.reshape(1, -1).astype(jnp.float32)
    bf16 = MXU_DTYPE
    kernel = functools.partial(_layer_kernel, d_att=d_att, d_conv=d_conv)
    tile_spec = pl.BlockSpec((None, TILE, d_model), lambda b, s: (b, s, 0))
    return pl.pallas_call(
        kernel,
        grid=(batch, seq // TILE),
        in_specs=[
            tile_spec,
            _const_spec((1, d_model)),
            _const_spec(w_in.shape),
            _const_spec((N_HEADS, BAND, TILE)),
            _const_spec(dw_w.shape),
            _const_spec((1, d_conv)),
            _const_spec((1, d_conv)),
            _const_spec((1, d_conv)),
            _const_spec(w_pw.shape),
            _const_spec((1, d_conv)),
            _const_spec(w_out.shape),
            _const_spec((1, d_model)),
        ],
        out_specs=tile_spec,
        out_shape=jax.ShapeDtypeStruct(x.shape, x.dtype),
        scratch_shapes=[
            pltpu.VMEM((BAND, d_att), bf16),
            pltpu.VMEM((N_SLOTS, d_att, TILE), bf16),
            pltpu.VMEM((CONV_HIST + TILE, d_conv), jnp.float32),
            pltpu.VMEM((N_HEADS, BAND, TILE), jnp.float32),
            pltpu.VMEM((d_att, TILE), jnp.float32),
            pltpu.VMEM((TILE, d_conv), jnp.float32),
        ],
        compiler_params=pltpu.CompilerParams(
            dimension_semantics=("arbitrary", "arbitrary"),
            vmem_limit_bytes=VMEM_LIMIT_BYTES),
        name="hymba_layer",
    )(x, row(pre_g), w_in.astype(bf16), _score_bias(rel_bias), dw_w.astype(jnp.float32),
      row(dw_b), row(ln_g), row(ln_b), w_pw.astype(bf16), row(b_pw), w_out.astype(bf16),
      row(post_g))


def kernel(x, pre_norm_g, w_in, rel_bias, dw_w, dw_b, conv_ln_g, conv_ln_b, w_pw, b_pw,
           w_out, post_norm_g):
    for l in range(pre_norm_g.shape[0]):
        x = _layer(x, pre_norm_g[l], w_in[l], rel_bias[l], dw_w[l], dw_b[l], conv_ln_g[l],
                   conv_ln_b[l], w_pw[l], b_pw[l], w_out[l], post_norm_g[l])
    return x
```

```python
import functools

import jax
import jax.numpy as jnp
from jax import lax
from jax.experimental import pallas as pl
from jax.experimental.pallas import tpu as pltpu

CHUNK = 64
LEFT_CHUNKS = 8
N_HEADS = 8
HEAD_DIM = 64
MAX_REL = 128
CONV_WIDTH = 31
EPS = 1e-6
MASK_VALUE = -1e30
LOG2_E = 1.4426950408889634
MXU_DTYPE = jnp.bfloat16

LANES = 128
SUBLANES = 8
BF16_SUBLANES = 16
VMEM_LIMIT_BYTES = 56 * 1024 * 1024

TILE = 4 * CHUNK
HIST = LEFT_CHUNKS * CHUNK
BAND = HIST + TILE
STEP_TILES = 2
N_BIAS_VARIANTS = HIST // TILE + 1
STEP = STEP_TILES * TILE
CONV_HIST = 32
HEADS_PER_SLAB = LANES // HEAD_DIM
CONV_ROW_SPLIT = 2
CONV_COL_SPLIT = 2
SCORE_LOOKAHEAD = 2


def _rmsnorm(v, g):
    ms = jnp.mean(v * v, axis=-1, keepdims=True)
    return v * lax.rsqrt(ms + EPS) * g


def _silu(v):
    return v * jax.nn.sigmoid(v)


def _layer_kernel(x_ref, pre_g_ref, w_in_ref, bias_ref, dw_w_ref, dw_b_ref,
                  ln_g_ref, ln_b_ref, w_pw_ref, b_pw_ref, w_out_ref, post_g_ref,
                  out_ref,
                  k_buf, vt_buf, hg_buf, at_buf, hc_buf, *, d_att, d_conv):
    s = pl.program_id(1)
    f32 = jnp.float32
    bf16 = MXU_DTYPE

    @pl.when(s == 0)
    def _():
        k_buf[0:HIST, :] = jnp.zeros((HIST, d_att), bf16)
        vt_buf[:, 0:HIST] = jnp.zeros((d_att, HIST), bf16)
        hg_buf[0:CONV_HIST, :] = jnp.zeros((CONV_HIST, d_conv), f32)

    c0 = 4 * d_att

    def tile(j):
        r0 = j * TILE
        variant = jnp.minimum(s * STEP_TILES + j, N_BIAS_VARIANTS - 1)
        x = x_ref[r0:r0 + TILE, :]
        h_in = _rmsnorm(x, pre_g_ref[...]).astype(bf16)

        def proj(i, width):
            return jnp.dot(h_in, w_in_ref[:, i:i + width], preferred_element_type=f32)

        def glu_part(part):
            width = d_conv // CONV_COL_SPLIT
            lo = part * width
            c_val = proj(c0 + lo, width)
            c_glu = proj(c0 + d_conv + lo, width)
            hg_buf[CONV_HIST + r0:CONV_HIST + r0 + TILE, lo:lo + width] = (
                c_val * jax.nn.sigmoid(c_glu))

        def conv_piece(piece):
            c, half = divmod(piece, CONV_ROW_SPLIT)
            rows = TILE // CONV_ROW_SPLIT
            i0 = half * rows
            cols = slice(c * LANES, (c + 1) * LANES)
            first = CONV_HIST - (CONV_WIDTH - 1)
            acc = jnp.broadcast_to(dw_b_ref[:, cols], (rows, LANES))
            for r in range(SUBLANES):
                n = rows + (SUBLANES if r else 0)
                part = None
                for t in range(CONV_WIDTH):
                    if (first + t) % SUBLANES != r:
                        continue
                    base = r0 + first + t - r + i0
                    term = dw_w_ref[t:t + 1, cols] * hg_buf[base:base + n, cols]
                    part = term if part is None else part + term
                acc = acc + part[r:r + rows]
            hc_buf[j, i0:i0 + rows, cols] = acc

        pieces_per_part = (d_conv // LANES) * CONV_ROW_SPLIT // CONV_COL_SPLIT
        for part in range(CONV_COL_SPLIT):
            glu_part(part)
            for piece in range(part * pieces_per_part, (part + 1) * pieces_per_part):
                conv_piece(piece)

        q = proj(0, d_att) * (HEAD_DIM ** -0.5 * LOG2_E)
        k = proj(d_att, d_att)
        v = proj(2 * d_att, d_att)
        k_buf[HIST + r0:HIST + r0 + TILE, :] = k.astype(bf16)
        vt_buf[:, HIST + r0:HIST + r0 + TILE] = v.T.astype(bf16)
        gate_att = _silu(proj(3 * d_att, d_att))
        gate_conv = _silu(proj(c0 + 2 * d_conv, d_conv))

        lane = lax.broadcasted_iota(jnp.int32, (TILE, d_att), 1)
        head_in_slab = (lane // HEAD_DIM) % HEADS_PER_SLAB
        q_bf = q.astype(bf16)
        q_sel = [jnp.where(head_in_slab == i, q_bf, jnp.zeros_like(q_bf))
                 for i in range(HEADS_PER_SLAB)]

        def scores_t(h):
            slab = slice((h // HEADS_PER_SLAB) * LANES, (h // HEADS_PER_SLAB + 1) * LANES)
            return lax.dot_general(k_buf[r0:r0 + BAND, slab], q_sel[h % HEADS_PER_SLAB][:, slab],
                                   (((1,), (1,)), ((), ())), preferred_element_type=f32)

        ones_rows = jnp.ones((BF16_SUBLANES, TILE), bf16)
        pending = [scores_t(h) for h in range(SCORE_LOOKAHEAD)]
        for h in range(N_HEADS):
            rows = slice(h * HEAD_DIM, (h + 1) * HEAD_DIM)
            st = pending.pop(0)
            if h + SCORE_LOOKAHEAD < N_HEADS:
                pending.append(scores_t(h + SCORE_LOOKAHEAD))
            st = [st[b * TILE:(b + 1) * TILE] + bias_ref[variant, h, b * TILE:(b + 1) * TILE, :]
                  for b in range(BAND // TILE)]
            m = functools.reduce(jnp.maximum, [jnp.max(t, axis=0, keepdims=True) for t in st])
            p = [jnp.exp2(t - m).astype(bf16) for t in st]
            ot = sum(jnp.dot(jnp.concatenate(
                [vt_buf[rows, r0 + b * TILE:r0 + (b + 1) * TILE], ones_rows], axis=0), p[b],
                preferred_element_type=f32) for b in range(BAND // TILE))
            at_buf[j, rows, :] = ot[:HEAD_DIM] / ot[HEAD_DIM:HEAD_DIM + 1]

        a = at_buf[j].T * gate_att

        hc = hc_buf[j]
        mu = jnp.mean(hc, axis=-1, keepdims=True)
        var = jnp.mean(jnp.square(hc - mu), axis=-1, keepdims=True)
        hn = (hc - mu) * lax.rsqrt(var + EPS) * ln_g_ref[...] + ln_b_ref[...]
        hs = _silu(hn).astype(bf16)
        pw = jnp.dot(hs, w_pw_ref[...], preferred_element_type=f32) + b_pw_ref[...]
        cmod = pw * gate_conv

        y = (jnp.dot(a.astype(bf16), w_out_ref[0:d_att, :], preferred_element_type=f32)
             + jnp.dot(cmod.astype(bf16), w_out_ref[d_att:d_att + d_conv, :],
                       preferred_element_type=f32))
        out_ref[r0:r0 + TILE, :] = x + _rmsnorm(y, post_g_ref[...])

    for j in range(STEP_TILES):
        pl.when(s > -1 - j)(functools.partial(tile, j))

    k_buf[0:HIST, :] = k_buf[STEP:STEP + HIST, :]
    vt_buf[:, 0:HIST] = vt_buf[:, STEP:STEP + HIST]
    hg_buf[0:CONV_HIST, :] = hg_buf[STEP:STEP + CONV_HIST, :]


def _bias_kernel(shifted_ref, out_ref):
    n_padding_keys = HIST - pl.program_id(0) * TILE
    q_chunk = lax.broadcasted_iota(jnp.int32, (SUBLANES, TILE), 1) // CHUNK
    for g in range(BAND // SUBLANES):
        off = BAND - 1 - SUBLANES * g
        key_chunk = (SUBLANES * g) // CHUNK
        in_band = (q_chunk <= key_chunk) & (q_chunk >= key_chunk - LEFT_CHUNKS)
        visible = in_band & (SUBLANES * g >= n_padding_keys)
        window = shifted_ref[:, off:off + TILE]
        out_ref[SUBLANES * g:SUBLANES * (g + 1), :] = jnp.where(visible, window * LOG2_E, MASK_VALUE)


def _score_bias(rel_bias):
    rb = rel_bias.astype(jnp.float32)
    n_heads = rb.shape[0]
    width = BAND + TILE
    lo = BAND - 1 - HIST - MAX_REL
    by_dist = jnp.concatenate([
        jnp.zeros((n_heads, SUBLANES - 1), jnp.float32),
        jnp.broadcast_to(rb[:, :1], (n_heads, lo)),
        rb,
        jnp.broadcast_to(rb[:, -1:], (n_heads, width - lo - rb.shape[1])),
    ], axis=1)
    shifted = jnp.stack([by_dist[:, SUBLANES - 1 - i:SUBLANES - 1 - i + width]
                         for i in range(SUBLANES)], axis=1)
    return pl.pallas_call(
        _bias_kernel,
        grid=(N_BIAS_VARIANTS, n_heads),
        in_specs=[pl.BlockSpec((None, SUBLANES, width), lambda v, h: (h, 0, 0))],
        out_specs=pl.BlockSpec((None, None, BAND, TILE), lambda v, h: (v, h, 0, 0)),
        out_shape=jax.ShapeDtypeStruct((N_BIAS_VARIANTS, n_heads, BAND, TILE), jnp.float32),
        compiler_params=pltpu.CompilerParams(dimension_semantics=("arbitrary", "arbitrary")),
        name="score_bias_table",
    )(shifted)


def _const_spec(shape):
    return pl.BlockSpec(shape, lambda b, s: (0,) * len(shape))


def _layer(x, pre_g, w_in, rel_bias, dw_w, dw_b, ln_g, ln_b, w_pw, b_pw, w_out, post_g):
    batch, seq, d_model = x.shape
    d_att = N_HEADS * HEAD_DIM
    d_conv = w_pw.shape[0]
    assert seq % STEP == 0 and STEP >= HIST and d_att % LANES == 0 and d_conv % LANES == 0
    assert w_in.shape == (d_model, 4 * d_att + 3 * d_conv)
    assert w_out.shape == (d_att + d_conv, d_model)
    row = lambda p: p.reshape(1, -1).astype(jnp.float32)
    bf16 = MXU_DTYPE
    kernel = functools.partial(_layer_kernel, d_att=d_att, d_conv=d_conv)
    step_spec = pl.BlockSpec((None, STEP, d_model), lambda b, s: (b, s, 0))
    return pl.pallas_call(
        kernel,
        grid=(batch, seq // STEP),
        in_specs=[
            step_spec,
            _const_spec((1, d_model)),
            _const_spec(w_in.shape),
            _const_spec((N_BIAS_VARIANTS, N_HEADS, BAND, TILE)),
            _const_spec(dw_w.shape),
            _const_spec((1, d_conv)),
            _const_spec((1, d_conv)),
            _const_spec((1, d_conv)),
            _const_spec(w_pw.shape),
            _const_spec((1, d_conv)),
            _const_spec(w_out.shape),
            _const_spec((1, d_model)),
        ],
        out_specs=step_spec,
        out_shape=jax.ShapeDtypeStruct(x.shape, x.dtype),
        scratch_shapes=[
            pltpu.VMEM((HIST + STEP, d_att), bf16),
            pltpu.VMEM((d_att, HIST + STEP), bf16),
            pltpu.VMEM((CONV_HIST + STEP, d_conv), jnp.float32),
            pltpu.VMEM((STEP_TILES, d_att, TILE), jnp.float32),
            pltpu.VMEM((STEP_TILES, TILE, d_conv), jnp.float32),
        ],
        compiler_params=pltpu.CompilerParams(
            dimension_semantics=("arbitrary", "arbitrary"),
            vmem_limit_bytes=VMEM_LIMIT_BYTES),
        name="hymba_layer",
    )(x, row(pre_g), w_in.astype(bf16), _score_bias(rel_bias), dw_w.astype(jnp.float32),
      row(dw_b), row(ln_g), row(ln_b), w_pw.astype(bf16), row(b_pw), w_out.astype(bf16),
      row(post_g))


def kernel(x, pre_norm_g, w_in, rel_bias, dw_w, dw_b, conv_ln_g, conv_ln_b, w_pw, b_pw,
           w_out, post_norm_g):
    for l in range(pre_norm_g.shape[0]):
        x = _layer(x, pre_norm_g[l], w_in[l], rel_bias[l], dw_w[l], dw_b[l], conv_ln_g[l],
                   conv_ln_b[l], w_pw[l], b_pw[l], w_out[l], post_norm_g[l])
    return x
```
